```python
import jax, jax.numpy as jnp
from jax import lax
import numpy as np

D_MODEL = 4096
BATCH = 16
SEQ = 256
DEPTH = 1
DEC_BATCH = 2
DEC_SEQ = 1024
PAST_LEN = 256

GRID_W = 64
EPS = 1e-6
GLA_HEADS = 4
GLA_WIDTH = D_MODEL // 2
GLA_DV = GLA_WIDTH // GLA_HEADS
GLA_DK = GLA_DV // 2
GLA_KW = GLA_HEADS * GLA_DK
GATE_RANK = 16
GATE_TAU = 16.0
CHUNK = 32
FNET_GROUPS = 4
FNET_WIDTH = D_MODEL // 2
FNET_GW = FNET_WIDTH // FNET_GROUPS
MIX_WIDTH = GLA_WIDTH + FNET_WIDTH
IN_COLS = 2 * GLA_KW + 2 * GLA_WIDTH + 2 * GATE_RANK + FNET_WIDTH
IN_SPLITS = (GLA_KW, 2 * GLA_KW, 2 * GLA_KW + GLA_WIDTH, 2 * GLA_KW + 2 * GLA_WIDTH,
             2 * GLA_KW + 2 * GLA_WIDTH + GATE_RANK, 2 * GLA_KW + 2 * GLA_WIDTH + 2 * GATE_RANK)
PEER_HEADS = 8
PEER_TOPK = 16
N_KEYS = 128
N_EXPERTS = N_KEYS * N_KEYS
PEER_DQ = 256
PEER_DH = PEER_DQ // 2
PEER_BLOCK = 64

kernel_name = "hybrid_gla_fnet_peer_diffusion_step"


def rmsnorm(x, g):
    xf = x.astype(jnp.float32)
    y = xf * lax.rsqrt(jnp.mean(xf * xf, axis=-1, keepdims=True) + EPS)
    return (y * g.astype(jnp.float32)).astype(x.dtype)


def pos_embed_2d(n_tok, d):
    rows = n_tok // GRID_W
    gr, gc = jnp.meshgrid(jnp.arange(rows, dtype=jnp.float32),
                          jnp.arange(GRID_W, dtype=jnp.float32), indexing="ij")
    gr, gc = gr.reshape(-1), gc.reshape(-1)
    quarter = d // 4
    omega = 1.0 / (10000.0 ** (jnp.arange(quarter, dtype=jnp.float32) / quarter))
    er = gr[:, None] * omega[None]
    ec = gc[:, None] * omega[None]
    return jnp.concatenate([jnp.sin(er), jnp.cos(er), jnp.sin(ec), jnp.cos(ec)], axis=-1)


def gla_chunked(q, k, v, log_a, s0):
    B, T, H, DK = q.shape
    DV = v.shape[-1]
    n = T // CHUNK
    def to_chunks(t):
        return t.astype(jnp.float32).reshape(B, n, CHUNK, H, t.shape[-1]).transpose(1, 0, 3, 2, 4)
    qc, kc, vc, la = to_chunks(q), to_chunks(k), to_chunks(v), to_chunks(log_a)
    bcum = jnp.cumsum(la, axis=3)
    mask = jnp.tril(jnp.ones((CHUNK, CHUNK), dtype=bool))

    def step(s, inp):
        qi, ki, vi, bi = inp
        o_inter = jnp.einsum("bhid,bhde->bhie", qi * jnp.exp(bi), s)
        diff = bi[:, :, :, None, :] - bi[:, :, None, :, :]
        decay = jnp.where(mask[None, None, :, :, None], jnp.exp(jnp.minimum(diff, 0.0)), 0.0)
        scores = jnp.einsum("bhid,bhjd,bhijd->bhij", qi, ki, decay)
        o_intra = jnp.einsum("bhij,bhje->bhie", scores, vi)
        b_last = bi[:, :, -1, :]
        s_new = s * jnp.exp(b_last)[..., None] + jnp.einsum(
            "bhjd,bhje->bhde", ki * jnp.exp(b_last[:, :, None, :] - bi), vi)
        return s_new, o_inter + o_intra

    s_fin, o = lax.scan(step, s0.astype(jnp.float32), (qc, kc, vc, bcum))
    o = o.transpose(1, 0, 3, 2, 4).reshape(B, T, H, DV)
    return o, s_fin


def fnet_mix(u):
    B, T, _ = u.shape
    uf = u.astype(jnp.float32).reshape(B, T, FNET_GROUPS, FNET_GW)
    y = jnp.fft.fft2(uf, axes=(1, 3), norm="ortho").real
    return y.reshape(B, T, FNET_WIDTH).astype(u.dtype)


def mixer_block(h, s0f, s0b, w_in, w_gf, b_gf, w_gb, b_gb, gla_g, w_out):
    B, T, _ = h.shape
    proj = h @ w_in
    q, k, v, r, af, ab, uf = jnp.split(proj, list(IN_SPLITS), axis=-1)
    q = q.reshape(B, T, GLA_HEADS, GLA_DK) * (GLA_DK ** -0.5)
    k = k.reshape(B, T, GLA_HEADS, GLA_DK)
    v = v.reshape(B, T, GLA_HEADS, GLA_DV)
    la_f = (jax.nn.log_sigmoid((af @ w_gf + b_gf).astype(jnp.float32)) / GATE_TAU).reshape(B, T, GLA_HEADS, GLA_DK)
    la_b = (jax.nn.log_sigmoid((ab @ w_gb + b_gb).astype(jnp.float32)) / GATE_TAU).reshape(B, T, GLA_HEADS, GLA_DK)
    o_f, s_f = gla_chunked(q, k, v, la_f, s0f)
    o_b, s_b = gla_chunked(jnp.flip(q, 1), jnp.flip(k, 1), jnp.flip(v, 1), jnp.flip(la_b, 1), s0b)
    o = o_f + jnp.flip(o_b, 1)
    o = o * lax.rsqrt(jnp.mean(o * o, axis=-1, keepdims=True) + EPS) * gla_g.astype(jnp.float32).reshape(GLA_HEADS, GLA_DV)
    o = (o.reshape(B, T, GLA_WIDTH) * jax.nn.silu(r.astype(jnp.float32))).astype(h.dtype)
    mixed = jnp.concatenate([o, fnet_mix(uf)], axis=-1) @ w_out
    return mixed, s_f, s_b


def peer(h, wq, keys, u_tab, v_tab):
    B, T, D = h.shape
    n = B * T
    xt = h.reshape(n, D)
    q = (xt @ wq).reshape(n, PEER_HEADS, 2, PEER_DH)
    s = jnp.einsum("nhpd,hpkd->nhpk", q, keys)
    v1, i1 = lax.top_k(s[:, :, 0], PEER_TOPK)
    v2, i2 = lax.top_k(s[:, :, 1], PEER_TOPK)
    cand = (v1[..., :, None] + v2[..., None, :]).reshape(n, PEER_HEADS, PEER_TOPK * PEER_TOPK)
    vals, ci = lax.top_k(cand, PEER_TOPK)
    e1 = jnp.take_along_axis(i1, ci // PEER_TOPK, axis=-1)
    e2 = jnp.take_along_axis(i2, ci % PEER_TOPK, axis=-1)
    idx = (e1 * N_KEYS + e2).reshape(n, PEER_HEADS * PEER_TOPK)
    g = jax.nn.softmax(vals.astype(jnp.float32), axis=-1).reshape(n, PEER_HEADS * PEER_TOPK).astype(h.dtype)
    nb = n // PEER_BLOCK

    def block(args):
        xb, ib, gb = args
        act = jax.nn.gelu(jnp.einsum("tkd,td->tk", u_tab[ib], xb), approximate=False)
        return jnp.einsum("tk,tkd->td", gb * act, v_tab[ib])

    out = lax.map(block, (xt.reshape(nb, PEER_BLOCK, D), idx.reshape(nb, PEER_BLOCK, -1),
                          g.reshape(nb, PEER_BLOCK, -1)))
    return out.reshape(B, T, D)


def layer(x, mod, s0f, s0b, n1, n2, w_in, w_gf, b_gf, w_gb, b_gb, gla_g, w_out, wq, keys, u_tab, v_tab):
    shift1, scale1, gate1, shift2, scale2, gate2 = jnp.split(mod[:, None, :].astype(x.dtype), 6, axis=-1)
    h = rmsnorm(x, n1) * (1 + scale1) + shift1
    m, s_f, s_b = mixer_block(h, s0f, s0b, w_in, w_gf, b_gf, w_gb, b_gb, gla_g, w_out)
    x = x + gate1 * m
    h2 = rmsnorm(x, n2) * (1 + scale2) + shift2
    x = x + gate2 * peer(h2, wq, keys, u_tab, v_tab)
    return x, s_f, s_b


def setup_inputs(seed: int = 0) -> dict:
    key = jax.random.key(seed)
    ks = jax.random.split(key, 24)
    f32 = jnp.float32
    nrm = lambda k, shape, s: jax.random.normal(k, shape, f32) * s
    return {
        "x_prompt": nrm(ks[0], (BATCH, SEQ, D_MODEL), 1.0),
        "x_sample": nrm(ks[1], (DEC_BATCH, DEC_SEQ, D_MODEL), 1.0),
        "state_gla_fwd": nrm(ks[2], (DEC_BATCH, DEPTH, GLA_HEADS, GLA_DK, GLA_DV), 1.0),
        "state_gla_bwd": nrm(ks[3], (DEC_BATCH, DEPTH, GLA_HEADS, GLA_DK, GLA_DV), 1.0),
        "c": nrm(ks[4], (DEC_BATCH, D_MODEL), 1.0),
        "c_ctx": nrm(ks[5], (D_MODEL,), 1.0),
        "norm1_g": 1.0 + nrm(ks[6], (DEPTH, D_MODEL), 0.02),
        "norm2_g": 1.0 + nrm(ks[7], (DEPTH, D_MODEL), 0.02),
        "w_ada": nrm(ks[8], (DEPTH, D_MODEL, 6 * D_MODEL), 0.5 * D_MODEL ** -0.5),
        "b_ada": nrm(ks[9], (DEPTH, 6 * D_MODEL), 0.02),
        "w_in": nrm(ks[10], (DEPTH, D_MODEL, IN_COLS), D_MODEL ** -0.5),
        "w_gate_fwd": nrm(ks[11], (DEPTH, GATE_RANK, GLA_KW), GATE_RANK ** -0.5),
        "b_gate_fwd": nrm(ks[12], (DEPTH, GLA_KW), 0.1),
        "w_gate_bwd": nrm(ks[13], (DEPTH, GATE_RANK, GLA_KW), GATE_RANK ** -0.5),
        "b_gate_bwd": nrm(ks[14], (DEPTH, GLA_KW), 0.1),
        "gla_norm_g": 1.0 + nrm(ks[15], (DEPTH, GLA_WIDTH), 0.02),
        "w_out": nrm(ks[16], (DEPTH, MIX_WIDTH, D_MODEL), MIX_WIDTH ** -0.5),
        "peer_wq": nrm(ks[17], (DEPTH, D_MODEL, PEER_HEADS * PEER_DQ), D_MODEL ** -0.5),
        "peer_keys": nrm(ks[18], (DEPTH, PEER_HEADS, 2, N_KEYS, PEER_DH), PEER_DH ** -0.5),
        "peer_u": nrm(ks[19], (DEPTH, N_EXPERTS, D_MODEL), D_MODEL ** -0.5),
        "peer_v": nrm(ks[20], (DEPTH, N_EXPERTS, D_MODEL), 0.5),
        "final_g": 1.0 + nrm(ks[21], (D_MODEL,), 0.02),
    }


def reference(x_prompt, x_sample, state_gla_fwd, state_gla_bwd, c, c_ctx, norm1_g, norm2_g, w_ada, b_ada,
              w_in, w_gate_fwd, b_gate_fwd, w_gate_bwd, b_gate_bwd, gla_norm_g, w_out,
              peer_wq, peer_keys, peer_u, peer_v, final_g):
    ctx = x_prompt
    lat = x_sample + pos_embed_2d(x_sample.shape[1], D_MODEL).astype(x_sample.dtype)[None]
    zero_state = jnp.zeros((x_prompt.shape[0], GLA_HEADS, GLA_DK, GLA_DV), jnp.float32)
    new_f, new_b = [], []
    for l in range(DEPTH):
        mod_ctx = (jax.nn.silu(c_ctx) @ w_ada[l] + b_ada[l])[None]
        mod_lat = jax.nn.silu(c) @ w_ada[l] + b_ada[l]
        params = (norm1_g[l], norm2_g[l], w_in[l], w_gate_fwd[l], b_gate_fwd[l], w_gate_bwd[l],
                  b_gate_bwd[l], gla_norm_g[l], w_out[l], peer_wq[l], peer_keys[l], peer_u[l], peer_v[l])
        ctx, s_f, s_b = layer(ctx, mod_ctx, zero_state, zero_state, *params)
        lat, _, _ = layer(lat, mod_lat, state_gla_fwd[:, l], state_gla_bwd[:, l], *params)
        new_f.append(s_f)
        new_b.append(s_b)
    y_prompt = rmsnorm(ctx, final_g)
    y_sample = rmsnorm(lat, final_g)
    new_state_gla_fwd = jnp.stack(new_f, axis=1)
    new_state_gla_bwd = jnp.stack(new_b, axis=1)
    return (y_prompt, y_sample, new_state_gla_fwd, new_state_gla_bwd)
```

```python
import functools
import math

import jax
import jax.numpy as jnp
from jax import lax
from jax.experimental import pallas as pl
from jax.experimental.pallas import tpu as pltpu

F32 = jnp.float32
BF16 = jnp.bfloat16

EPS = 1e-6
GLA_HEADS = 4
GATE_RANK = 16
GATE_TAU = 16.0
FNET_GROUPS = 4
PEER_HEADS = 8
PEER_TOPK = 16
N_KEYS = 128
GRID_W = 64

LANES = 128
VMEM_LIMIT_BYTES = 56 * 1024 * 1024

GLA_EXP_CLAMP = 60.0
GLA_CHUNK = 64

ROW_SHIFT1, ROW_SCALE1, ROW_GATE1, ROW_SHIFT2, ROW_SCALE2, ROW_GATE2 = range(6)
MOD_ROWS = 8

_NT = (((1,), (1,)), ((), ()))
_TN = (((0,), (0,)), ((), ()))


def _params(*sem):
    return pltpu.CompilerParams(dimension_semantics=sem, vmem_limit_bytes=VMEM_LIMIT_BYTES)


def _segment_of_tile(i, tm, n_ctx, lat_seq):
    start = i * tm
    return jnp.where(start < n_ctx, 0, 1 + (start - n_ctx) // lat_seq)


def _ada_kernel(c_ref, w_ref, b_ref, o_ref):
    c = c_ref[...]
    s = (c * jax.nn.sigmoid(c)).astype(BF16)
    o_ref[...] = jnp.dot(s, w_ref[...].astype(BF16), preferred_element_type=F32) + b_ref[...]


def _ada(cvec, w_ada, b_ada, tn=512):
    rows, d = cvec.shape
    n_out = w_ada.shape[1]
    return pl.pallas_call(
        _ada_kernel,
        grid=(n_out // tn,),
        in_specs=[
            pl.BlockSpec((rows, d), lambda j: (0, 0)),
            pl.BlockSpec((d, tn), lambda j: (0, j)),
            pl.BlockSpec((1, tn), lambda j: (0, j)),
        ],
        out_specs=pl.BlockSpec((rows, tn), lambda j: (0, j)),
        out_shape=jax.ShapeDtypeStruct((rows, n_out), F32),
        compiler_params=_params("arbitrary"),
        name="ada",
    )(cvec, w_ada, b_ada.reshape(1, n_out))


NORM_ROWS = 32


def _norm_modulate(x_ref, g_ref, mod_ref, h_ref, shift_row, scale_row):
    g = g_ref[...]
    scale = 1.0 + mod_ref[0, scale_row:scale_row + 1, :]
    shift = mod_ref[0, shift_row:shift_row + 1, :]
    n_trips = x_ref.shape[0] // NORM_ROWS

    def body(r, carry):
        rows = pl.ds(pl.multiple_of(r * NORM_ROWS, NORM_ROWS), NORM_ROWS)
        x = x_ref[rows, :]
        ms = jnp.mean(x * x, axis=-1, keepdims=True)
        y = (x * lax.rsqrt(ms + EPS)) * g
        h_ref[rows, :] = (y * scale + shift).astype(h_ref.dtype)
        return carry

    lax.fori_loop(0, n_trips, body, 0)


def _in_proj_kernel(x_ref, g_ref, mod_ref, w_ref, wg_ref, o_ref, og_ref, h_ref):
    @pl.when(pl.program_id(1) == 0)
    def _():
        _norm_modulate(x_ref, g_ref, mod_ref, h_ref, ROW_SHIFT1, ROW_SCALE1)
        og_ref[...] = jnp.dot(h_ref[...], wg_ref[...], preferred_element_type=F32).astype(og_ref.dtype)

    o_ref[...] = jnp.dot(h_ref[...], w_ref[...], preferred_element_type=F32).astype(o_ref.dtype)


def _in_proj(x, g, mod, w, wg, *, n_ctx, lat_seq, tm=512, tn=1024):
    n, d = x.shape
    n_out = w.shape[1]
    seg = functools.partial(_segment_of_tile, tm=tm, n_ctx=n_ctx, lat_seq=lat_seq)
    return pl.pallas_call(
        _in_proj_kernel,
        grid=(n // tm, n_out // tn),
        in_specs=[
            pl.BlockSpec((tm, d), lambda i, j: (i, 0)),
            pl.BlockSpec((1, d), lambda i, j: (0, 0)),
            pl.BlockSpec((1, MOD_ROWS, d), lambda i, j: (seg(i), 0, 0)),
            pl.BlockSpec((d, tn), lambda i, j: (0, j)),
            pl.BlockSpec((d, LANES), lambda i, j: (0, 0)),
        ],
        out_specs=[
            pl.BlockSpec((tm, tn), lambda i, j: (i, j)),
            pl.BlockSpec((tm, LANES), lambda i, j: (i, 0)),
        ],
        out_shape=[
            jax.ShapeDtypeStruct((n, n_out), BF16),
            jax.ShapeDtypeStruct((n, LANES), BF16),
        ],
        scratch_shapes=[pltpu.VMEM((tm, d), BF16)],
        compiler_params=_params("arbitrary", "arbitrary"),
        name="in_proj",
    )(x, g.reshape(1, d), mod, w, wg)


def _q_proj_kernel(x_ref, g_ref, mod_ref, w_ref, o_ref, h_ref):
    @pl.when(pl.program_id(1) == 0)
    def _():
        _norm_modulate(x_ref, g_ref, mod_ref, h_ref, ROW_SHIFT2, ROW_SCALE2)

    o_ref[...] = jnp.dot(h_ref[...], w_ref[...], preferred_element_type=F32).astype(o_ref.dtype)


def _q_proj(x, g, mod, w, *, n_ctx, lat_seq, tm=512, tn=1024):
    n, d = x.shape
    n_out = w.shape[1]
    seg = functools.partial(_segment_of_tile, tm=tm, n_ctx=n_ctx, lat_seq=lat_seq)
    return pl.pallas_call(
        _q_proj_kernel,
        grid=(n // tm, n_out // tn),
        in_specs=[
            pl.BlockSpec((tm, d), lambda i, j: (i, 0)),
            pl.BlockSpec((1, d), lambda i, j: (0, 0)),
            pl.BlockSpec((1, MOD_ROWS, d), lambda i, j: (seg(i), 0, 0)),
            pl.BlockSpec((d, tn), lambda i, j: (0, j)),
        ],
        out_specs=[
            pl.BlockSpec((tm, tn), lambda i, j: (i, j)),
            pl.BlockSpec((tm, d), lambda i, j: (i, 0)),
        ],
        out_shape=[
            jax.ShapeDtypeStruct((n, n_out), BF16),
            jax.ShapeDtypeStruct((n, d), BF16),
        ],
        compiler_params=_params("arbitrary", "arbitrary"),
        name="q_proj",
    )(x, g.reshape(1, d), mod, w)


def _gla_kernel(*refs, seq, chunk, has_init, emit_state):
    q_ref, k_ref, v_ref, r_ref, gate_ref, wgf_ref, bgf_ref, wgb_ref, bgb_ref, gain_ref = refs[:10]
    pos = 10
    if has_init:
        s0f_ref, s0b_ref = refs[pos:pos + 2]
        pos += 2
    o_ref = refs[pos]
    pos += 1
    if emit_state:
        sf_ref, sb_ref = refs[pos:pos + 2]
        pos += 2
    acc_ref, st_ref = refs[pos:pos + 2]

    n_chunks = seq // chunk
    dk = q_ref.shape[1]
    q_scale = dk ** -0.5
    row_i = lax.broadcasted_iota(jnp.int32, (chunk, chunk), 0)
    col_i = lax.broadcasted_iota(jnp.int32, (chunk, chunk), 1)
    gain = gain_ref[...]

    def scan(reverse, wg_ref, bg_ref):
        tri = (col_i >= row_i) if reverse else (col_i <= row_i)
        tri_f = tri.astype(F32)
        wg = wg_ref[...]
        bg = bg_ref[...]

        def body(i, carry):
            c = (n_chunks - 1 - i) if reverse else i
            rows = pl.ds(pl.multiple_of(c * chunk, chunk), chunk)
            q = q_ref[rows, :].astype(F32) * q_scale
            k = k_ref[rows, :].astype(F32)
            v = v_ref[rows, :]
            x = jnp.dot(gate_ref[rows, :], wg, preferred_element_type=F32) + bg
            log_a = (jnp.minimum(x, 0.0) - jnp.log(1.0 + jnp.exp(-jnp.abs(x)))) * (1.0 / GATE_TAU)
            b = jnp.dot(tri_f, log_a, preferred_element_type=F32, precision=lax.Precision.HIGHEST)
            b_tot = b[0:1, :] if reverse else b[chunk - 1:chunk, :]
            b_mid = b[chunk // 2:chunk // 2 + 1, :]
            st = st_ref[...]
            o_inter = lax.dot_general((q * jnp.exp(b)).astype(BF16), st.astype(BF16), _NT,
                                      preferred_element_type=F32)
            qe = (q * jnp.exp(jnp.minimum(b - b_mid, GLA_EXP_CLAMP))).astype(BF16)
            ke = (k * jnp.exp(jnp.minimum(b_mid - b, GLA_EXP_CLAMP))).astype(BF16)
            scores = lax.dot_general(qe, ke, _NT, preferred_element_type=F32)
            scores = jnp.where(tri, scores, 0.0).astype(BF16)
            o = o_inter + jnp.dot(scores, v, preferred_element_type=F32)
            kd = (k * jnp.exp(b_tot - b)).astype(BF16)
            st_ref[...] = st * jnp.exp(b_tot) + lax.dot_general(v, kd, _TN, preferred_element_type=F32)
            if reverse:
                o = o + acc_ref[rows, :]
                o = o * lax.rsqrt(jnp.mean(o * o, axis=-1, keepdims=True) + EPS) * gain
                r = r_ref[rows, :].astype(F32)
                o_ref[rows, :] = (o * (r * jax.nn.sigmoid(r))).astype(o_ref.dtype)
            else:
                acc_ref[rows, :] = o
            return carry

        lax.fori_loop(0, n_chunks, body, 0)

    for reverse in (False, True):
        if has_init:
            st_ref[...] = (s0b_ref if reverse else s0f_ref)[0, 0]
        else:
            st_ref[...] = jnp.zeros_like(st_ref)
        scan(reverse, wgb_ref if reverse else wgf_ref, bgb_ref if reverse else bgf_ref)
        if emit_state:
            (sb_ref if reverse else sf_ref)[0, 0] = st_ref[...].T


def _gla(proj, gates, wgf, bgf, wgb, bgb, gain, init_states, *, batch, seq, row_block0, emit_state,
         chunk=GLA_CHUNK):
    heads = GLA_HEADS
    dk = wgf.shape[1] // heads
    dv = gain.shape[1] // heads
    kq_blocks = heads
    v_block0 = 2 * heads * dk // dv
    r_block0 = v_block0 + heads
    has_init = init_states is not None

    in_specs = [
        pl.BlockSpec((seq, dk), lambda b, h: (row_block0 + b, h)),
        pl.BlockSpec((seq, dk), lambda b, h: (row_block0 + b, kq_blocks + h)),
        pl.BlockSpec((seq, dv), lambda b, h: (row_block0 + b, v_block0 + h)),
        pl.BlockSpec((seq, dv), lambda b, h: (row_block0 + b, r_block0 + h)),
        pl.BlockSpec((seq, LANES), lambda b, h: (row_block0 + b, 0)),
        pl.BlockSpec((LANES, dk), lambda b, h: (0, h)),
        pl.BlockSpec((1, dk), lambda b, h: (0, h)),
        pl.BlockSpec((LANES, dk), lambda b, h: (0, h)),
        pl.BlockSpec((1, dk), lambda b, h: (0, h)),
        pl.BlockSpec((1, dv), lambda b, h: (0, h)),
    ]
    args = [proj, proj, proj, proj, gates, wgf, bgf, wgb, bgb, gain]
    if has_init:
        in_specs += [pl.BlockSpec((1, 1, dv, dk), lambda b, h: (b, h, 0, 0))] * 2
        args += list(init_states)
    out_specs = [pl.BlockSpec((seq, dv), lambda b, h: (b, h))]
    out_shape = [jax.ShapeDtypeStruct((batch * seq, heads * dv), BF16)]
    if emit_state:
        out_specs += [pl.BlockSpec((1, 1, dk, dv), lambda b, h: (b, h, 0, 0))] * 2
        out_shape += [jax.ShapeDtypeStruct((batch, heads, dk, dv), F32)] * 2
    return pl.pallas_call(
        functools.partial(_gla_kernel, seq=seq, chunk=chunk, has_init=has_init, emit_state=emit_state),
        grid=(batch, heads),
        in_specs=in_specs,
        out_specs=out_specs,
        out_shape=out_shape,
        scratch_shapes=[pltpu.VMEM((seq, dv), F32), pltpu.VMEM((dv, dk), F32)],
        compiler_params=_params("arbitrary", "arbitrary"),
        name="gla",
    )(*args)


def _dft_cos_sin(n):
    j = lax.broadcasted_iota(jnp.int32, (n, n), 0)
    k = lax.broadcasted_iota(jnp.int32, (n, n), 1)
    ang = ((j * k) % n).astype(F32) * (2.0 * math.pi / n)
    return jnp.cos(ang), jnp.sin(ang)


def _fnet_kernel(u_ref, cs_ref, ct_ref, st_ref, o_ref, *, scale):
    gw = u_ref.shape[1]
    xc = jnp.dot(u_ref[...], cs_ref[...], preferred_element_type=F32).astype(BF16)
    y = (jnp.dot(ct_ref[...], xc[:, :gw], preferred_element_type=F32)
         - jnp.dot(st_ref[...], xc[:, gw:], preferred_element_type=F32))
    o_ref[...] = (y * scale).astype(o_ref.dtype)


def _fnet(proj, *, batch, seq, row_block0, col_block0, gw):
    cos_c, sin_c = _dft_cos_sin(gw)
    cos_t, sin_t = _dft_cos_sin(seq)
    cs = jnp.concatenate([cos_c, sin_c], axis=1).astype(BF16)
    return pl.pallas_call(
        functools.partial(_fnet_kernel, scale=1.0 / math.sqrt(seq * gw)),
        grid=(batch, FNET_GROUPS),
        in_specs=[
            pl.BlockSpec((seq, gw), lambda b, g: (row_block0 + b, col_block0 + g)),
            pl.BlockSpec((gw, 2 * gw), lambda b, g: (0, 0)),
            pl.BlockSpec((seq, seq), lambda b, g: (0, 0)),
            pl.BlockSpec((seq, seq), lambda b, g: (0, 0)),
        ],
        out_specs=pl.BlockSpec((seq, gw), lambda b, g: (b, g)),
        out_shape=jax.ShapeDtypeStruct((batch * seq, FNET_GROUPS * gw), BF16),
        compiler_params=_params("arbitrary", "arbitrary"),
        name="fnet",
    )(proj, cs, cos_t.astype(BF16), sin_t.astype(BF16))


def _out_proj_kernel(a_ref, f_ref, wa_ref, wf_ref, x_ref, mod_ref, y_ref):
    mixed = (jnp.dot(a_ref[...], wa_ref[...], preferred_element_type=F32)
             + jnp.dot(f_ref[...], wf_ref[...], preferred_element_type=F32))
    y_ref[...] = x_ref[...] + mod_ref[0, ROW_GATE1:ROW_GATE1 + 1, :] * mixed


def _out_proj(attn, four, w_attn, w_four, x, mod, *, n_ctx, lat_seq, tm=512, tn=1024):
    n, d = x.shape
    ka = attn.shape[1]
    kf = four.shape[1]
    seg = functools.partial(_segment_of_tile, tm=tm, n_ctx=n_ctx, lat_seq=lat_seq)
    return pl.pallas_call(
        _out_proj_kernel,
        grid=(n // tm, d // tn),
        in_specs=[
            pl.BlockSpec((tm, ka), lambda i, j: (i, 0)),
            pl.BlockSpec((tm, kf), lambda i, j: (i, 0)),
            pl.BlockSpec((ka, tn), lambda i, j: (0, j)),
            pl.BlockSpec((kf, tn), lambda i, j: (0, j)),
            pl.BlockSpec((tm, tn), lambda i, j: (i, j)),
            pl.BlockSpec((1, MOD_ROWS, tn), lambda i, j: (seg(i), 0, j)),
        ],
        out_specs=pl.BlockSpec((tm, tn), lambda i, j: (i, j)),
        out_shape=jax.ShapeDtypeStruct((n, d), F32),
        compiler_params=_params("arbitrary", "arbitrary"),
        name="out_proj",
    )(attn, four, w_attn, w_four, x, mod)


def _top_values_desc(s, count, out_ref):
    n_rows = s.shape[0]
    row = lax.broadcasted_iota(jnp.int32, s.shape, 0)
    for r in range(count):
        m = jnp.max(s, axis=0, keepdims=True)
        first = jnp.min(jnp.where(s == m, row, n_rows), axis=0, keepdims=True)
        s = jnp.where(row == first, -jnp.inf, s)
        out_ref[r:r + 1, :] = m


def _peer_select_kernel(q_ref, keys_ref, s_ref, p_ref, tau_ref, v1_ref, v2_ref, best_ref):
    q = q_ref[...]
    dh = q.shape[1] // 2
    s1 = lax.dot_general(keys_ref[0, 0], q[:, :dh], _NT, preferred_element_type=F32)
    s2 = lax.dot_general(keys_ref[0, 1], q[:, dh:], _NT, preferred_element_type=F32)
    _top_values_desc(s1, PEER_TOPK, v1_ref)
    _top_values_desc(s2, PEER_TOPK, v2_ref)
    half = PEER_TOPK // 2
    v2_top = v2_ref[0:half, :]
    cand = [v1_ref[0:1, :] + v2_ref[...]]
    cand += [v1_ref[a:a + 1, :] + v2_top for a in range(1, half)]
    cand += [v1_ref[half:, :] + v2_ref[0:1, :]]
    _top_values_desc(jnp.concatenate(cand, axis=0), PEER_TOPK, best_ref)
    best = best_ref[...]
    z = jnp.sum(jnp.exp(best - best[0:1, :]), axis=0, keepdims=True)
    n_keys = s1.shape[0]
    s_ref[0, :n_keys, :] = s1
    s_ref[0, n_keys:, :] = s2
    p_ref[0, :n_keys, :] = jnp.exp(s1 - v1_ref[0:1, :])
    p_ref[0, n_keys:, :] = jnp.exp(s2 - v2_ref[0:1, :]) / z
    tau_ref[0] = best[PEER_TOPK - 1:PEER_TOPK, :]


def _peer_select(q, keys, tm=256):
    n = q.shape[0]
    heads, _, n_keys, dh = keys.shape
    stat = jax.ShapeDtypeStruct((heads, 2 * n_keys, n), F32)
    return pl.pallas_call(
        _peer_select_kernel,
        grid=(n // tm, heads),
        in_specs=[
            pl.BlockSpec((tm, 2 * dh), lambda i, h: (i, h)),
            pl.BlockSpec((1, 2, n_keys, dh), lambda i, h: (h, 0, 0, 0)),
        ],
        out_specs=[
            pl.BlockSpec((1, 2 * n_keys, tm), lambda i, h: (h, 0, i)),
            pl.BlockSpec((1, 2 * n_keys, tm), lambda i, h: (h, 0, i)),
            pl.BlockSpec((1, 1, tm), lambda i, h: (h, 0, i)),
        ],
        out_shape=[stat, stat, jax.ShapeDtypeStruct((heads, 1, n), F32)],
        scratch_shapes=[pltpu.VMEM((PEER_TOPK, tm), F32)] * 3,
        compiler_params=_params("arbitrary", "arbitrary"),
        name="peer_select",
    )(q, keys)


def _peer_experts_kernel(h_ref, u_ref, v_ref, s_ref, p_ref, tau_ref, o_ref):
    j = pl.program_id(1)

    @pl.when(j == 0)
    def _():
        o_ref[...] = jnp.zeros_like(o_ref)

    heads = s_ref.shape[0]
    n_keys = s_ref.shape[1] // 2
    te = u_ref.shape[0]
    act = lax.dot_general(u_ref[...], h_ref[...], _NT, preferred_element_type=F32)
    gelu = 0.5 * act * (1.0 + lax.erf(act * (1.0 / math.sqrt(2.0))))
    blocks = []
    for a in range(te // n_keys):
        e1 = j * (te // n_keys) + a
        w = None
        for h in range(heads):
            s1 = s_ref[h, pl.ds(e1, 1), :]
            p1 = p_ref[h, pl.ds(e1, 1), :]
            s2 = s_ref[h, n_keys:, :]
            p2 = p_ref[h, n_keys:, :]
            term = jnp.where(s1 + s2 >= tau_ref[h], p2, 0.0) * p1
            w = term if w is None else w + term
        blocks.append(w)
    weighted = jnp.concatenate(blocks, axis=0) * gelu
    o_ref[...] += jnp.dot(weighted.T.astype(BF16), v_ref[...], preferred_element_type=F32)


def _peer_experts(h, u, v, s, p, tau, tm=512, te=512):
    n, d = h.shape
    n_exp = u.shape[0]
    heads, rows, _ = s.shape
    once = pl.Buffered(1)
    return pl.pallas_call(
        _peer_experts_kernel,
        grid=(n // tm, n_exp // te),
        in_specs=[
            pl.BlockSpec((tm, d), lambda i, j: (i, 0), pipeline_mode=once),
            pl.BlockSpec((te, d), lambda i, j: (j, 0)),
            pl.BlockSpec((te, d), lambda i, j: (j, 0)),
            pl.BlockSpec((heads, rows, tm), lambda i, j: (0, 0, i), pipeline_mode=once),
            pl.BlockSpec((heads, rows, tm), lambda i, j: (0, 0, i), pipeline_mode=once),
            pl.BlockSpec((heads, 1, tm), lambda i, j: (0, 0, i), pipeline_mode=once),
        ],
        out_specs=pl.BlockSpec((tm, d), lambda i, j: (i, 0)),
        out_shape=jax.ShapeDtypeStruct((n, d), F32),
        compiler_params=_params("arbitrary", "arbitrary"),
        name="peer_experts",
    )(h, u, v, s, p, tau)


def _final_kernel(x_ref, e_ref, mod_ref, g_ref, y_ref):
    x = x_ref[...] + mod_ref[0, ROW_GATE2:ROW_GATE2 + 1, :] * e_ref[...]
    ms = jnp.mean(x * x, axis=-1, keepdims=True)
    y_ref[...] = (x * lax.rsqrt(ms + EPS)) * g_ref[...]


def _final(x, e, mod, g, *, n_ctx, lat_seq, tm=128):
    n, d = x.shape
    seg = functools.partial(_segment_of_tile, tm=tm, n_ctx=n_ctx, lat_seq=lat_seq)
    return pl.pallas_call(
        _final_kernel,
        grid=(n // tm,),
        in_specs=[
            pl.BlockSpec((tm, d), lambda i: (i, 0)),
            pl.BlockSpec((tm, d), lambda i: (i, 0)),
            pl.BlockSpec((1, MOD_ROWS, d), lambda i: (seg(i), 0, 0)),
            pl.BlockSpec((1, d), lambda i: (0, 0)),
        ],
        out_specs=pl.BlockSpec((tm, d), lambda i: (i, 0)),
        out_shape=jax.ShapeDtypeStruct((n, d), F32),
        compiler_params=_params("arbitrary"),
        name="final",
    )(x, e, mod, g.reshape(1, d))


def _pos_embed_2d(n_tok, d):
    rows = n_tok // GRID_W
    gr, gc = jnp.meshgrid(jnp.arange(rows, dtype=F32), jnp.arange(GRID_W, dtype=F32), indexing="ij")
    gr, gc = gr.reshape(-1), gc.reshape(-1)
    quarter = d // 4
    omega = 1.0 / (10000.0 ** (jnp.arange(quarter, dtype=F32) / quarter))
    er = gr[:, None] * omega[None]
    ec = gc[:, None] * omega[None]
    return jnp.concatenate([jnp.sin(er), jnp.cos(er), jnp.sin(ec), jnp.cos(ec)], axis=-1)


def _pad_gate_weight(w, row0):
    rank, cols = w.shape
    return jnp.zeros((LANES, cols), BF16).at[row0:row0 + rank].set(w.astype(BF16))


def kernel(x_prompt, x_sample, state_gla_fwd, state_gla_bwd, c, c_ctx, norm1_g, norm2_g, w_ada, b_ada,
           w_in, w_gate_fwd, b_gate_fwd, w_gate_bwd, b_gate_bwd, gla_norm_g, w_out,
           peer_wq, peer_keys, peer_u, peer_v, final_g):
    batch, seq, d = x_prompt.shape
    lat_batch, lat_seq, _ = x_sample.shape
    depth = norm1_g.shape[0]
    assert depth == 1, "the PEER residual is fused with the final norm, which assumes a single layer"
    n_ctx = batch * seq
    n_lat = lat_batch * lat_seq
    kw = w_gate_fwd.shape[2]
    vw = gla_norm_g.shape[1]
    fw = w_in.shape[2] - 2 * kw - 2 * vw - 2 * GATE_RANK
    main_cols = 2 * kw + 2 * vw
    dv = vw // GLA_HEADS
    gw = fw // FNET_GROUPS
    shared = dict(n_ctx=n_ctx, lat_seq=lat_seq)

    lat = x_sample + _pos_embed_2d(lat_seq, d)[None]
    x = jnp.concatenate([x_prompt.reshape(n_ctx, d), lat.reshape(n_lat, d)], axis=0)

    cvec = jnp.concatenate([c_ctx[None], c, jnp.zeros((MOD_ROWS - 1 - lat_batch, d), F32)], axis=0)
    new_f, new_b = [], []
    for l in range(depth):
        mod = _ada(cvec, w_ada[l], b_ada[l])[:1 + lat_batch].reshape(1 + lat_batch, 6, d)
        mod = jnp.pad(mod, ((0, 0), (0, MOD_ROWS - 6), (0, 0)))

        w_main = jnp.concatenate([w_in[l][:, :main_cols], w_in[l][:, main_cols + 2 * GATE_RANK:]],
                                 axis=1).astype(BF16)
        w_gates = jnp.pad(w_in[l][:, main_cols:main_cols + 2 * GATE_RANK],
                          ((0, 0), (0, LANES - 2 * GATE_RANK))).astype(BF16)
        proj, gates = _in_proj(x, norm1_g[l], mod, w_main, w_gates, **shared)

        wgf = _pad_gate_weight(w_gate_fwd[l], 0)
        wgb = _pad_gate_weight(w_gate_bwd[l], GATE_RANK)
        gla_args = (proj, gates, wgf, b_gate_fwd[l][None], wgb, b_gate_bwd[l][None], gla_norm_g[l][None])
        attn_ctx, s_f, s_b = _gla(*gla_args, None, batch=batch, seq=seq, row_block0=0, emit_state=True)
        init = (jnp.swapaxes(state_gla_fwd[:, l], -1, -2), jnp.swapaxes(state_gla_bwd[:, l], -1, -2))
        (attn_lat,) = _gla(*gla_args, init, batch=lat_batch, seq=lat_seq, row_block0=n_ctx // lat_seq,
                           emit_state=False)
        new_f.append(s_f)
        new_b.append(s_b)

        four_ctx = _fnet(proj, batch=batch, seq=seq, row_block0=0, col_block0=main_cols // gw, gw=gw)
        four_lat = _fnet(proj, batch=lat_batch, seq=lat_seq, row_block0=n_ctx // lat_seq,
                         col_block0=main_cols // gw, gw=gw)

        attn = jnp.concatenate([attn_ctx, attn_lat], axis=0)
        four = jnp.concatenate([four_ctx, four_lat], axis=0)
        w_out_b = w_out[l].astype(BF16)
        x = _out_proj(attn, four, w_out_b[:vw], w_out_b[vw:], x, mod, **shared)

        q, h2 = _q_proj(x, norm2_g[l], mod, peer_wq[l].astype(BF16), **shared)
        s, p, tau = _peer_select(q, peer_keys[l].astype(BF16))
        e = _peer_experts(h2, peer_u[l].astype(BF16), peer_v[l].astype(BF16), s, p, tau)
        y = _final(x, e, mod, final_g, **shared)

    y_prompt = y[:n_ctx].reshape(batch, seq, d)
    y_sample = y[n_ctx:].reshape(lat_batch, lat_seq, d)
    return (y_prompt, y_sample, jnp.stack(new_f, axis=1), jnp.stack(new_b, axis=1))
```
